```python
import math
import jax, jax.numpy as jnp
from jax import lax
import numpy as np

D_MODEL = 1024
BATCH = 8
SEQ = 2048
DEPTH = 2
DEC_BATCH = 32
DEC_SEQ = 4
PAST_LEN = 16384
PAGE_SIZE = 128

N_MIXERS = 2
N_MLA_LAYERS = (DEPTH + 1) // 2
N_SWA_LAYERS = DEPTH // 2
RMS_EPS = 1e-6
ROPE_THETA = 10000.0
Q_BLOCK = 128
MLA_HEADS = 8
Q_LORA = 384
KV_LORA = 256
QK_NOPE = 128
QK_ROPE = 64
V_HEAD = 128
MLA_ROW = KV_LORA + QK_ROPE
MLA_SCALE = 1.0 / math.sqrt(QK_NOPE + QK_ROPE)
SWA_HEADS = 16
SWA_KV_HEADS = 4
SWA_GROUP = SWA_HEADS // SWA_KV_HEADS
SWA_HEAD_DIM = 64
WINDOW = 128
SWA_SCALE = 1.0 / math.sqrt(SWA_HEAD_DIM)
N_GROUPS = 4
EXPERTS_PER_GROUP = 8
N_EXPERTS = N_GROUPS * EXPERTS_PER_GROUP
TOP_K_IN_GROUP = 2
D_FF_EXPERT = 512
MOE_BLOCK = 128

kernel_name = 'hybrid_mla_swa_hmoe_step'


def rms_norm(x, g):
    xf = x.astype(jnp.float32)
    y = xf * lax.rsqrt(jnp.mean(xf * xf, axis=-1, keepdims=True) + RMS_EPS)
    return (y * g.astype(jnp.float32)).astype(x.dtype)


def rope(x, pos):
    half = x.shape[-1] // 2
    inv_freq = ROPE_THETA ** (-jnp.arange(half, dtype=jnp.float32) / half)
    ang = pos.astype(jnp.float32)[:, None] * inv_freq[None, :]
    cos = jnp.cos(ang)[:, None, :]
    sin = jnp.sin(ang)[:, None, :]
    xf = x.astype(jnp.float32)
    x1, x2 = xf[..., :half], xf[..., half:]
    return jnp.concatenate([x1 * cos - x2 * sin, x2 * cos + x1 * sin], axis=-1).astype(x.dtype)


def mla_project(h, pos, w_a, q_norm, wq_b, kv_norm):
    b, s, _ = h.shape
    a = h @ w_a
    cq = rms_norm(a[..., :Q_LORA], q_norm)
    c = rms_norm(a[..., Q_LORA:Q_LORA + KV_LORA], kv_norm)
    k_rope = rope(a[..., None, Q_LORA + KV_LORA:], pos)[:, :, 0]
    q = (cq @ wq_b).reshape(b, s, MLA_HEADS, QK_NOPE + QK_ROPE)
    return q[..., :QK_NOPE], rope(q[..., QK_NOPE:], pos), c, k_rope


def split_wkv_b(wkv_b):
    w = wkv_b.reshape(KV_LORA, MLA_HEADS, QK_NOPE + V_HEAD)
    return w[..., :QK_NOPE], w[..., QK_NOPE:]


def mla_prompt(h, w_a, q_norm, wq_b, kv_norm, wkv_b, wo):
    b, s, _ = h.shape
    pos = jnp.arange(s, dtype=jnp.int32)
    q_nope, q_rope, c, k_rope = mla_project(h, pos, w_a, q_norm, wq_b, kv_norm)
    w_uk, w_uv = split_wkv_b(wkv_b)
    k_nope = jnp.einsum('bsc,chd->bshd', c, w_uk)
    v = jnp.einsum('bsc,chd->bshd', c, w_uv)
    nb = s // Q_BLOCK
    qn_blocks = q_nope.reshape(b, nb, Q_BLOCK, MLA_HEADS, QK_NOPE).swapaxes(0, 1)
    qr_blocks = q_rope.reshape(b, nb, Q_BLOCK, MLA_HEADS, QK_ROPE).swapaxes(0, 1)
    key_pos = jnp.arange(s)

    def attend_block(args):
        qn, qr, blk = args
        sc = (jnp.einsum('bqhd,bkhd->bhqk', qn, k_nope).astype(jnp.float32)
              + jnp.einsum('bqhr,bkr->bhqk', qr, k_rope).astype(jnp.float32)) * MLA_SCALE
        q_pos = blk * Q_BLOCK + jnp.arange(Q_BLOCK)
        sc = jnp.where(key_pos[None, :] <= q_pos[:, None], sc, -jnp.inf)
        p = jax.nn.softmax(sc, axis=-1).astype(v.dtype)
        return jnp.einsum('bhqk,bkhd->bqhd', p, v)

    o = lax.map(attend_block, (qn_blocks, qr_blocks, jnp.arange(nb)))
    o = o.swapaxes(0, 1).reshape(b, s, MLA_HEADS * V_HEAD)
    rows = jnp.concatenate([c, k_rope], axis=-1)
    return o @ wo, rows


def mla_sample(h, cache_pool, page_table, w_a, q_norm, wq_b, kv_norm, wkv_b, wo):
    b, t, _ = h.shape
    past_len = page_table.shape[1] * PAGE_SIZE
    pos = past_len + jnp.arange(t, dtype=jnp.int32)
    q_nope, q_rope, c, k_rope = mla_project(h, pos, w_a, q_norm, wq_b, kv_norm)
    w_uk, w_uv = split_wkv_b(wkv_b)
    past = cache_pool[page_table].reshape(b, past_len, MLA_ROW)
    c_past, kr_past = past[..., :KV_LORA], past[..., KV_LORA:]
    q_lat = jnp.einsum('bthd,chd->bthc', q_nope, w_uk)
    s_past = (jnp.einsum('bthc,bkc->bhtk', q_lat, c_past).astype(jnp.float32)
              + jnp.einsum('bthr,bkr->bhtk', q_rope, kr_past).astype(jnp.float32)) * MLA_SCALE
    s_new = (jnp.einsum('bthc,bkc->bhtk', q_lat, c).astype(jnp.float32)
             + jnp.einsum('bthr,bkr->bhtk', q_rope, k_rope).astype(jnp.float32)) * MLA_SCALE
    causal = jnp.tril(jnp.ones((t, t), dtype=bool))
    s_new = jnp.where(causal, s_new, -jnp.inf)
    p = jax.nn.softmax(jnp.concatenate([s_past, s_new], axis=-1), axis=-1).astype(c.dtype)
    o_lat = (jnp.einsum('bhtk,bkc->bthc', p[..., :past_len], c_past)
             + jnp.einsum('bhtk,bkc->bthc', p[..., past_len:], c))
    o = jnp.einsum('bthc,chd->bthd', o_lat, w_uv).reshape(b, t, MLA_HEADS * V_HEAD)
    rows = jnp.concatenate([c, k_rope], axis=-1)
    return o @ wo, rows


def swa_project(h, pos, wqkv, bqkv):
    b, s, _ = h.shape
    nq = SWA_HEADS * SWA_HEAD_DIM
    nk = SWA_KV_HEADS * SWA_HEAD_DIM
    qkv = h @ wqkv + bqkv
    q = qkv[..., :nq].reshape(b, s, SWA_HEADS, SWA_HEAD_DIM)
    k = qkv[..., nq:nq + nk].reshape(b, s, SWA_KV_HEADS, SWA_HEAD_DIM)
    v = qkv[..., nq + nk:].reshape(b, s, SWA_KV_HEADS, SWA_HEAD_DIM)
    return rope(q, pos), rope(k, pos), v


def sink_softmax(sc, sink):
    m = jnp.maximum(jnp.max(sc, axis=-1, keepdims=True), sink)
    e = jnp.exp(sc - m)
    return e / (jnp.sum(e, axis=-1, keepdims=True) + jnp.exp(sink - m))


def swa_prompt(h, wqkv, bqkv, sinks, wo, bo):
    b, s, _ = h.shape
    pos = jnp.arange(s, dtype=jnp.int32)
    q, k, v = swa_project(h, pos, wqkv, bqkv)
    nb = s // WINDOW
    qb = q.reshape(b, nb, WINDOW, SWA_KV_HEADS, SWA_GROUP, SWA_HEAD_DIM)

    def band(z):
        cur = z.reshape(b, nb, WINDOW, SWA_KV_HEADS, SWA_HEAD_DIM)
        prev = jnp.concatenate([jnp.zeros_like(cur[:, :1]), cur[:, :-1]], axis=1)
        return jnp.concatenate([prev, cur], axis=2)

    kb, vb = band(k), band(v)
    sc = jnp.einsum('bnqhgd,bnshd->bnhgqs', qb, kb).astype(jnp.float32) * SWA_SCALE
    blk = jnp.arange(nb)[:, None, None] * WINDOW
    q_pos = blk + jnp.arange(WINDOW)[None, :, None]
    k_pos = blk - WINDOW + jnp.arange(2 * WINDOW)[None, None, :]
    dist = q_pos - k_pos
    valid = (dist >= 0) & (dist < WINDOW) & (k_pos >= 0)
    sc = jnp.where(valid[None, :, None, None], sc, -jnp.inf)
    sink = sinks.astype(jnp.float32).reshape(SWA_KV_HEADS, SWA_GROUP)[None, None, :, :, None, None]
    p = sink_softmax(sc, sink).astype(v.dtype)
    o = jnp.einsum('bnhgqs,bnshd->bnqhgd', p, vb).reshape(b, s, SWA_HEADS * SWA_HEAD_DIM)
    state = jnp.stack([k[:, -WINDOW:], v[:, -WINDOW:]], axis=2)
    return o @ wo + bo, state


def swa_sample(h, buf, wqkv, bqkv, sinks, wo, bo):
    b, t, _ = h.shape
    pos = PAST_LEN + jnp.arange(t, dtype=jnp.int32)
    q, k, v = swa_project(h, pos, wqkv, bqkv)
    kc = jnp.concatenate([buf[:, :, 0], k], axis=1)
    vc = jnp.concatenate([buf[:, :, 1], v], axis=1)
    qg = q.reshape(b, t, SWA_KV_HEADS, SWA_GROUP, SWA_HEAD_DIM)
    sc = jnp.einsum('bthgd,bshd->bhgts', qg, kc).astype(jnp.float32) * SWA_SCALE
    k_rel = jnp.arange(WINDOW + t) - WINDOW
    dist = jnp.arange(t)[:, None] - k_rel[None, :]
    valid = (dist >= 0) & (dist < WINDOW)
    sc = jnp.where(valid, sc, -jnp.inf)
    sink = sinks.astype(jnp.float32).reshape(SWA_KV_HEADS, SWA_GROUP)[None, :, :, None, None]
    p = sink_softmax(sc, sink).astype(vc.dtype)
    o = jnp.einsum('bhgts,bshd->bthgd', p, vc).reshape(b, t, SWA_HEADS * SWA_HEAD_DIM)
    state = jnp.stack([kc[:, -WINDOW:], vc[:, -WINDOW:]], axis=2)
    return o @ wo + bo, state


def hier_moe(x, w_group, b_group, w_expert, b_expert, w_gate, w_up, w_down):
    shape = x.shape
    xt = x.reshape(-1, shape[-1])
    n_tok = xt.shape[0]
    g_logits = (xt @ w_group).astype(jnp.float32) + b_group.astype(jnp.float32)
    g_prob = jax.nn.softmax(g_logits, axis=-1)
    g_p, g_idx = lax.top_k(g_prob, 1)
    e_logits = ((xt @ w_expert).astype(jnp.float32) + b_expert.astype(jnp.float32)).reshape(
        n_tok, N_GROUPS, EXPERTS_PER_GROUP)
    e_sel = e_logits[jnp.arange(n_tok), g_idx[:, 0]]
    e_top, e_idx = lax.top_k(e_sel, TOP_K_IN_GROUP)
    gate = g_p * jax.nn.softmax(e_top, axis=-1)
    expert_id = g_idx * EXPERTS_PER_GROUP + e_idx

    n_slots = n_tok * TOP_K_IN_GROUP
    eid = expert_id.reshape(n_slots).astype(jnp.int32)
    tok = jnp.repeat(jnp.arange(n_tok, dtype=jnp.int32), TOP_K_IN_GROUP)
    wt = gate.reshape(n_slots)
    order = jnp.argsort(eid)
    sorted_e = eid[order]
    counts = jnp.bincount(eid, length=N_EXPERTS)
    padded = (counts + MOE_BLOCK - 1) // MOE_BLOCK * MOE_BLOCK
    seg_start = jnp.cumsum(counts) - counts
    pad_end = jnp.cumsum(padded)
    pad_start = pad_end - padded
    dest = pad_start[sorted_e] + jnp.arange(n_slots) - seg_start[sorted_e]
    n_rows = -(-n_slots // MOE_BLOCK) * MOE_BLOCK + N_EXPERTS * MOE_BLOCK
    row_tok = jnp.zeros((n_rows,), jnp.int32).at[dest].set(tok[order])
    row_w = jnp.zeros((n_rows,), jnp.float32).at[dest].set(wt[order])
    n_blk = n_rows // MOE_BLOCK
    blk_e = jnp.minimum(jnp.searchsorted(pad_end, jnp.arange(n_blk) * MOE_BLOCK, side='right'),
                        N_EXPERTS - 1)
    xr = xt[row_tok].reshape(n_blk, MOE_BLOCK, shape[-1])

    def run_block(args):
        xb, e = args
        hid = jax.nn.silu(xb @ w_gate[e]) * (xb @ w_up[e])
        return hid @ w_down[e]

    yr = lax.map(run_block, (xr, blk_e)).reshape(n_rows, shape[-1])
    yr = yr.astype(jnp.float32) * row_w[:, None]
    y = jnp.zeros((n_tok, shape[-1]), jnp.float32).at[row_tok].add(yr)
    return y.astype(x.dtype).reshape(shape)


def setup_inputs(seed: int = 0) -> dict:
    key = jax.random.key(seed)
    ks = iter(jax.random.split(key, 32))
    f32 = jnp.float32

    def dense(shape, fan_in):
        return jax.random.normal(next(ks), shape, f32) * fan_in ** -0.5

    def gain(shape):
        return 1.0 + 0.02 * jax.random.normal(next(ks), shape, f32)

    def small(shape, scale):
        return scale * jax.random.normal(next(ks), shape, f32)

    n_pages = PAST_LEN // PAGE_SIZE
    n_used = DEC_BATCH * n_pages
    n_phys = (5 * n_used + 3) // 4
    x_prompt = jax.random.normal(next(ks), (BATCH, SEQ, D_MODEL), f32)
    x_sample = jax.random.normal(next(ks), (DEC_BATCH, DEC_SEQ, D_MODEL), f32)
    cache_mla = jax.random.normal(next(ks), (N_MLA_LAYERS, n_phys, PAGE_SIZE, MLA_ROW), f32)
    cache_swa_kv = jax.random.normal(
        next(ks), (N_SWA_LAYERS, DEC_BATCH, WINDOW, 2, SWA_KV_HEADS, SWA_HEAD_DIM), f32)
    page_table = jax.random.permutation(next(ks), n_phys)[:n_used].reshape(
        DEC_BATCH, n_pages).astype(jnp.int32)
    qkv_w = (SWA_HEADS + 2 * SWA_KV_HEADS) * SWA_HEAD_DIM
    return {
        'x_prompt': x_prompt,
        'x_sample': x_sample,
        'cache_mla': cache_mla,
        'cache_swa_kv': cache_swa_kv,
        'page_table': page_table,
        'norm_mix': gain((DEPTH, D_MODEL)),
        'norm_ffn': gain((DEPTH, D_MODEL)),
        'norm_final': gain((D_MODEL,)),
        'mla_w_a': dense((N_MLA_LAYERS, D_MODEL, Q_LORA + KV_LORA + QK_ROPE), D_MODEL),
        'mla_q_norm': gain((N_MLA_LAYERS, Q_LORA)),
        'mla_wq_b': dense((N_MLA_LAYERS, Q_LORA, MLA_HEADS * (QK_NOPE + QK_ROPE)), Q_LORA),
        'mla_kv_norm': gain((N_MLA_LAYERS, KV_LORA)),
        'mla_wkv_b': dense((N_MLA_LAYERS, KV_LORA, MLA_HEADS * (QK_NOPE + V_HEAD)), KV_LORA),
        'mla_wo': dense((N_MLA_LAYERS, MLA_HEADS * V_HEAD, D_MODEL), MLA_HEADS * V_HEAD),
        'swa_wqkv': dense((N_SWA_LAYERS, D_MODEL, qkv_w), D_MODEL),
        'swa_bqkv': small((N_SWA_LAYERS, qkv_w), 0.02),
        'swa_sinks': small((N_SWA_LAYERS, SWA_HEADS), 0.5),
        'swa_wo': dense((N_SWA_LAYERS, SWA_HEADS * SWA_HEAD_DIM, D_MODEL), SWA_HEADS * SWA_HEAD_DIM),
        'swa_bo': small((N_SWA_LAYERS, D_MODEL), 0.02),
        'moe_w_group': dense((DEPTH, D_MODEL, N_GROUPS), D_MODEL),
        'moe_b_group': small((DEPTH, N_GROUPS), 0.01),
        'moe_w_expert': dense((DEPTH, D_MODEL, N_EXPERTS), D_MODEL),
        'moe_b_expert': small((DEPTH, N_EXPERTS), 0.01),
        'moe_w_gate': dense((DEPTH, N_EXPERTS, D_MODEL, D_FF_EXPERT), D_MODEL),
        'moe_w_up': dense((DEPTH, N_EXPERTS, D_MODEL, D_FF_EXPERT), D_MODEL),
        'moe_w_down': dense((DEPTH, N_EXPERTS, D_FF_EXPERT, D_MODEL), D_FF_EXPERT),
    }


def reference(x_prompt, x_sample, cache_mla, cache_swa_kv, page_table, norm_mix, norm_ffn,
              norm_final, mla_w_a, mla_q_norm, mla_wq_b, mla_kv_norm, mla_wkv_b, mla_wo,
              swa_wqkv, swa_bqkv, swa_sinks, swa_wo, swa_bo, moe_w_group, moe_b_group,
              moe_w_expert, moe_b_expert, moe_w_gate, moe_w_up, moe_w_down):
    xp, xs = x_prompt, x_sample
    mla_p, mla_s, swa_p, swa_s = [], [], [], []
    for layer in range(DEPTH):
        j = layer // N_MIXERS
        hp = rms_norm(xp, norm_mix[layer])
        hs = rms_norm(xs, norm_mix[layer])
        if layer % N_MIXERS == 0:
            w = (mla_w_a[j], mla_q_norm[j], mla_wq_b[j], mla_kv_norm[j], mla_wkv_b[j], mla_wo[j])
            dp, rp = mla_prompt(hp, *w)
            ds, rs = mla_sample(hs, cache_mla[j], page_table, *w)
            mla_p.append(rp)
            mla_s.append(rs)
        else:
            w = (swa_wqkv[j], swa_bqkv[j], swa_sinks[j], swa_wo[j], swa_bo[j])
            dp, rp = swa_prompt(hp, *w)
            ds, rs = swa_sample(hs, cache_swa_kv[j], *w)
            swa_p.append(rp)
            swa_s.append(rs)
        xp = xp + dp
        xs = xs + ds
        moe_w = (moe_w_group[layer], moe_b_group[layer], moe_w_expert[layer], moe_b_expert[layer],
                 moe_w_gate[layer], moe_w_up[layer], moe_w_down[layer])
        xp = xp + hier_moe(rms_norm(xp, norm_ffn[layer]), *moe_w)
        xs = xs + hier_moe(rms_norm(xs, norm_ffn[layer]), *moe_w)
    y_prompt = rms_norm(xp, norm_final)
    y_sample = rms_norm(xs, norm_final)
    mla_rows_prompt = jnp.stack(mla_p)
    mla_rows_sample = jnp.stack(mla_s)
    swa_kv_prompt = jnp.stack(swa_p)
    swa_kv_sample = jnp.stack(swa_s)
    return (y_prompt, y_sample, mla_rows_prompt, mla_rows_sample, swa_kv_prompt, swa_kv_sample)
```

```python
import functools
import math

import jax
import jax.numpy as jnp
from jax import lax
from jax.experimental import pallas as pl
from jax.experimental.pallas import tpu as pltpu

F32 = jnp.float32
BF16 = jnp.bfloat16
I32 = jnp.int32

D_MODEL = 1024
RMS_EPS = 1e-6
ROPE_THETA = 10000.0
PAGE_SIZE = 128
MLA_HEADS = 8
Q_LORA = 384
KV_LORA = 256
QK_NOPE = 128
QK_ROPE = 64
V_HEAD = 128
MLA_ROW = KV_LORA + QK_ROPE
MLA_SCALE = 1.0 / math.sqrt(QK_NOPE + QK_ROPE)
SWA_HEADS = 16
SWA_KV_HEADS = 4
SWA_HEAD_DIM = 64
WINDOW = 128
SWA_SCALE = 1.0 / math.sqrt(SWA_HEAD_DIM)
N_GROUPS = 4
EXPERTS_PER_GROUP = 8
N_EXPERTS = N_GROUPS * EXPERTS_PER_GROUP
D_FF_EXPERT = 512

LANES = 128
TM = 512
MOE_BM = 256
MLA_TQ = 512
PAGES_PER_STEP = 16
VMEM_LIMIT = 56 * 1024 * 1024
NEG = -1e30


def _cparams(sem):
    return pltpu.CompilerParams(dimension_semantics=sem, vmem_limit_bytes=VMEM_LIMIT)


def _rms(x, g):
    return x * lax.rsqrt(jnp.mean(x * x, axis=-1, keepdims=True) + RMS_EPS) * g


def _rope(x, c, sa, sb):
    return x * c + pltpu.roll(x, 96, 1) * sa + pltpu.roll(x, 32, 1) * sb


def _dot(a, b):
    return jnp.dot(a, b, preferred_element_type=F32)


def _dot_nt(a, b):
    return lax.dot_general(a, b, (((1,), (1,)), ((), ())), preferred_element_type=F32)


def _mla_proj_kernel(xp_ref, xs_ref, g_ref, wa_ref, qn_ref, wq_ref, kvn_ref, wuk_ref, wuv_ref,
                     c_ref, sa_ref, sb_ref,
                     q_ref, kn_ref, kr_ref, v_ref, rowsp_ref, rowss_ref, *, n_main):
    i = pl.program_id(0)
    x = jnp.where(i < n_main, xp_ref[...], xs_ref[...])
    h = _rms(x, g_ref[...])
    a = _dot(h.astype(BF16), wa_ref[...])
    cq = _rms(a[:, :Q_LORA], qn_ref[...])
    c = _rms(a[:, Q_LORA:Q_LORA + KV_LORA], kvn_ref[...])
    ct, sa, sb = c_ref[...], sa_ref[...], sb_ref[...]
    kr = _rope(a[:, Q_LORA + KV_LORA:], ct, sa, sb)
    q = _dot(cq.astype(BF16), wq_ref[...])
    for hd in range(MLA_HEADS):
        lo = hd * 2 * LANES
        q_ref[:, lo:lo + LANES] = (q[:, lo:lo + LANES] * MLA_SCALE).astype(BF16)
        qr = _rope(q[:, lo + LANES:lo + 2 * LANES], ct, sa, sb)
        q_ref[:, lo + LANES:lo + 2 * LANES] = (qr * MLA_SCALE).astype(BF16)
    cb = c.astype(BF16)
    kn_ref[...] = _dot(cb, wuk_ref[...]).astype(BF16)
    v_ref[...] = _dot(cb, wuv_ref[...]).astype(BF16)
    kr_ref[...] = kr.astype(BF16)

    def write_rows(ref):
        ref[:, :KV_LORA] = c
        ref[:, KV_LORA:] = kr[:, :QK_ROPE]

    @pl.when(i < n_main)
    def _():
        write_rows(rowsp_ref)

    @pl.when(i >= n_main)
    def _():
        write_rows(rowss_ref)


def _mla_proj(xp, xs, g, wa, qn, wq, kvn, wuk, wuv, ct, sa, sb):
    n_main = xp.shape[0] // TM
    n_tiles = n_main + 1
    t_pad = n_tiles * TM
    row = lambda i: (i, 0)
    main = lambda i: (jnp.minimum(i, n_main - 1), 0)
    fixed = lambda i: (0, 0)
    full = lambda a: pl.BlockSpec(a.shape, fixed)
    return pl.pallas_call(
        functools.partial(_mla_proj_kernel, n_main=n_main),
        grid=(n_tiles,),
        in_specs=[pl.BlockSpec((TM, D_MODEL), main), pl.BlockSpec((TM, D_MODEL), fixed),
                  full(g), full(wa), full(qn), full(wq), full(kvn), full(wuk), full(wuv),
                  pl.BlockSpec((TM, LANES), row), pl.BlockSpec((TM, LANES), row),
                  pl.BlockSpec((TM, LANES), row)],
        out_specs=[pl.BlockSpec((TM, MLA_HEADS * 2 * LANES), row),
                   pl.BlockSpec((TM, MLA_HEADS * QK_NOPE), row),
                   pl.BlockSpec((TM, LANES), row),
                   pl.BlockSpec((TM, MLA_HEADS * V_HEAD), row),
                   pl.BlockSpec((TM, MLA_ROW), main),
                   pl.BlockSpec((TM, MLA_ROW), fixed)],
        out_shape=[jax.ShapeDtypeStruct((t_pad, MLA_HEADS * 2 * LANES), BF16),
                   jax.ShapeDtypeStruct((t_pad, MLA_HEADS * QK_NOPE), BF16),
                   jax.ShapeDtypeStruct((t_pad, LANES), BF16),
                   jax.ShapeDtypeStruct((t_pad, MLA_HEADS * V_HEAD), BF16),
                   jax.ShapeDtypeStruct((n_main * TM, MLA_ROW), F32),
                   jax.ShapeDtypeStruct((TM, MLA_ROW), F32)],
        compiler_params=_cparams(("arbitrary",)),
        name="mla_proj",
    )(xp, xs, g, wa, qn, wq, kvn, wuk, wuv, ct, sa, sb)


def _mla_attn_kernel(q_ref, kn_ref, kr_ref, v_ref, o_ref, m_ref, l_ref, acc_ref):
    qi = pl.program_id(2)
    t = MLA_TQ
    q = q_ref[...]
    m_ref[...] = jnp.full_like(m_ref, NEG)
    l_ref[...] = jnp.zeros_like(l_ref)
    acc_ref[...] = jnp.zeros_like(acc_ref)

    def step(j, diagonal):
        ks = pl.multiple_of(j * t, t)
        k = jnp.concatenate([kn_ref[pl.ds(ks, t), :], kr_ref[pl.ds(ks, t), :]], axis=1)
        s = _dot_nt(q, k)
        if diagonal:
            r = lax.broadcasted_iota(I32, (t, t), 0)
            cidx = lax.broadcasted_iota(I32, (t, t), 1)
            s = jnp.where(cidx <= r, s, NEG)
        m_prev = m_ref[...]
        m_new = jnp.maximum(m_prev, jnp.max(s, axis=-1, keepdims=True))
        alpha = jnp.exp(m_prev - m_new)
        p = jnp.exp(s - m_new)
        l_ref[...] = alpha * l_ref[...] + jnp.sum(p, axis=-1, keepdims=True)
        acc_ref[...] = alpha * acc_ref[...] + _dot(p.astype(BF16), v_ref[pl.ds(ks, t), :])
        m_ref[...] = m_new

    def body(j, carry):
        step(j, False)
        return carry

    lax.fori_loop(0, qi, body, 0)
    step(qi, True)
    o_ref[...] = (acc_ref[...] / l_ref[...]).astype(BF16)


def _mla_attn_prompt(q, kn, kr, v, batch, seq):
    nq = seq // MLA_TQ
    return pl.pallas_call(
        _mla_attn_kernel,
        grid=(batch, MLA_HEADS, nq),
        in_specs=[pl.BlockSpec((MLA_TQ, 2 * LANES), lambda b, h, i: (b * nq + i, h)),
                  pl.BlockSpec((seq, QK_NOPE), lambda b, h, i: (b, h)),
                  pl.BlockSpec((seq, LANES), lambda b, h, i: (b, 0)),
                  pl.BlockSpec((seq, V_HEAD), lambda b, h, i: (b, h))],
        out_specs=pl.BlockSpec((MLA_TQ, V_HEAD), lambda b, h, i: (b * nq + i, h)),
        out_shape=jax.ShapeDtypeStruct((batch * seq, MLA_HEADS * V_HEAD), BF16),
        scratch_shapes=[pltpu.VMEM((MLA_TQ, 1), F32), pltpu.VMEM((MLA_TQ, 1), F32),
                        pltpu.VMEM((MLA_TQ, V_HEAD), F32)],
        compiler_params=_cparams(("arbitrary", "arbitrary", "arbitrary")),
        name="mla_attn_prompt",
    )(q, kn, kr, v)


def _mla_q_absorb_kernel(q_ref, w_ref, o_ref):
    q = q_ref[...]
    o_ref[0, :, :KV_LORA] = _dot(q[:, :QK_NOPE], w_ref[0]).astype(BF16)
    o_ref[0, :, KV_LORA:] = q[:, QK_NOPE:]


def _mla_q_absorb(q, w_ukt, sample_tile):
    return pl.pallas_call(
        _mla_q_absorb_kernel,
        grid=(MLA_HEADS,),
        in_specs=[pl.BlockSpec((TM, 2 * LANES), lambda h: (sample_tile, h)),
                  pl.BlockSpec((1, QK_NOPE, KV_LORA), lambda h: (h, 0, 0))],
        out_specs=pl.BlockSpec((1, TM, KV_LORA + LANES), lambda h: (h, 0, 0)),
        out_shape=jax.ShapeDtypeStruct((MLA_HEADS, TM, KV_LORA + LANES), BF16),
        compiler_params=_cparams(("arbitrary",)),
        name="mla_q_absorb",
    )(q, w_ukt)


def _mla_sample_kernel(pt_ref, q_ref, new_ref, cache_ref, o_ref, buf, sem, m_ref, l_ref, acc_ref,
                       *, n_pages, n_chunks, dec_seq):
    b = pl.program_id(0)
    c = pl.program_id(1)
    pg = PAGES_PER_STEP
    step = b * n_chunks + c
    total = pl.num_programs(0) * n_chunks
    slot = step % 2

    def page_copy(bb, cc, k, sl):
        page = pt_ref[bb * n_pages + cc * pg + k]
        return pltpu.make_async_copy(cache_ref.at[page], buf.at[sl, k], sem.at[sl])

    def fetch(bb, cc, sl):
        for k in range(pg):
            page_copy(bb, cc, k, sl).start()

    @pl.when(step == 0)
    def _():
        fetch(0, 0, 0)

    @pl.when(step + 1 < total)
    def _():
        nxt = step + 1
        fetch(nxt // n_chunks, nxt % n_chunks, 1 - slot)

    q = q_ref[0]
    rows = q.shape[0]

    @pl.when(c == 0)
    def _():
        qf = q[:, :MLA_ROW].astype(F32)
        new = new_ref[0].astype(BF16).astype(F32)
        tok = jnp.bitwise_and(lax.broadcasted_iota(I32, (rows, 1), 0), dec_seq - 1)
        s_new = [jnp.where(tok >= u, jnp.sum(qf * new[u:u + 1, :], axis=-1, keepdims=True), NEG)
                 for u in range(dec_seq)]
        m = s_new[0]
        for u in range(1, dec_seq):
            m = jnp.maximum(m, s_new[u])
        l = jnp.zeros_like(m)
        acc = jnp.zeros((rows, KV_LORA), F32)
        for u in range(dec_seq):
            p = jnp.exp(s_new[u] - m)
            l = l + p
            acc = acc + p.astype(BF16).astype(F32) * new[u:u + 1, :KV_LORA]
        m_ref[...] = m
        l_ref[...] = l
        acc_ref[...] = acc

    for k in range(pg):
        page_copy(b, c, k, slot).wait()

    kv = buf[slot].reshape(pg * PAGE_SIZE, MLA_ROW).astype(BF16)
    s = _dot_nt(q[:, :MLA_ROW], kv)
    m_prev = m_ref[...]
    m_new = jnp.maximum(m_prev, jnp.max(s, axis=-1, keepdims=True))
    alpha = jnp.exp(m_prev - m_new)
    p = jnp.exp(s - m_new)
    l_ref[...] = alpha * l_ref[...] + jnp.sum(p, axis=-1, keepdims=True)
    acc_ref[...] = alpha * acc_ref[...] + _dot(p.astype(BF16), kv[:, :KV_LORA])
    m_ref[...] = m_new

    @pl.when(c == n_chunks - 1)
    def _():
        o_ref[0] = acc_ref[...] / l_ref[...]


def _mla_attn_sample(page_table, qcat, new_rows, cache):
    dec_batch, n_pages = page_table.shape
    dec_seq = new_rows.shape[1]
    rows = qcat.shape[1]
    n_chunks = n_pages // PAGES_PER_STEP
    kern = functools.partial(_mla_sample_kernel, n_pages=n_pages, n_chunks=n_chunks,
                             dec_seq=dec_seq)
    return pl.pallas_call(
        kern,
        grid_spec=pltpu.PrefetchScalarGridSpec(
            num_scalar_prefetch=1,
            grid=(dec_batch, n_chunks),
            in_specs=[pl.BlockSpec((1, rows, KV_LORA + LANES), lambda b, c, pt: (b, 0, 0)),
                      pl.BlockSpec((1, dec_seq, MLA_ROW), lambda b, c, pt: (b, 0, 0)),
                      pl.BlockSpec(memory_space=pl.ANY)],
            out_specs=pl.BlockSpec((1, rows, KV_LORA), lambda b, c, pt: (b, 0, 0)),
            scratch_shapes=[pltpu.VMEM((2, PAGES_PER_STEP, PAGE_SIZE, MLA_ROW), F32),
                            pltpu.SemaphoreType.DMA((2,)),
                            pltpu.VMEM((rows, 1), F32), pltpu.VMEM((rows, 1), F32),
                            pltpu.VMEM((rows, KV_LORA), F32)]),
        out_shape=jax.ShapeDtypeStruct((dec_batch, rows, KV_LORA), F32),
        compiler_params=_cparams(("arbitrary", "arbitrary")),
        name="mla_attn_sample",
    )(page_table.reshape(-1), qcat, new_rows, cache)


def _mla_v_up_kernel(o_ref, w_ref, out_ref):
    out_ref[...] = _dot(o_ref[0].astype(BF16), w_ref[0]).astype(BF16)


def _mla_v_up(o_lat, w_uv):
    return pl.pallas_call(
        _mla_v_up_kernel,
        grid=(MLA_HEADS,),
        in_specs=[pl.BlockSpec((1, TM, KV_LORA), lambda h: (h, 0, 0)),
                  pl.BlockSpec((1, KV_LORA, V_HEAD), lambda h: (h, 0, 0))],
        out_specs=pl.BlockSpec((TM, V_HEAD), lambda h: (0, h)),
        out_shape=jax.ShapeDtypeStruct((TM, MLA_HEADS * V_HEAD), BF16),
        compiler_params=_cparams(("arbitrary",)),
        name="mla_v_up",
    )(o_lat, w_uv)


def _route(xn, whi_ref, wlo_ref, br_ref, cnt_ref, ri_ref, rg_ref):
    rows = xn.shape[0]
    xh = xn.astype(BF16)
    xl = (xn - xh.astype(F32)).astype(BF16)
    logits = (_dot(xh, whi_ref[...]) + _dot(xl, whi_ref[...]) + _dot(xh, wlo_ref[...])
              + br_ref[...])
    lane = lax.broadcasted_iota(I32, (rows, LANES), 1)
    big = jnp.int32(LANES)
    is_g = (lane >= N_EXPERTS) & (lane < N_EXPERTS + N_GROUPS)
    gl = jnp.where(is_g, logits, -jnp.inf)
    gmax = jnp.max(gl, axis=-1, keepdims=True)
    gidx = jnp.min(jnp.where(gl == gmax, lane - N_EXPERTS, big), axis=-1, keepdims=True)
    g_p = 1.0 / jnp.sum(jnp.exp(gl - gmax), axis=-1, keepdims=True)
    in_group = (lane < N_EXPERTS) & (jnp.right_shift(lane, 3) == gidx)
    el = jnp.where(in_group, logits, -jnp.inf)
    e1 = jnp.max(el, axis=-1, keepdims=True)
    i1 = jnp.min(jnp.where(el == e1, lane, big), axis=-1, keepdims=True)
    el2 = jnp.where(lane == i1, -jnp.inf, el)
    e2 = jnp.max(el2, axis=-1, keepdims=True)
    i2 = jnp.min(jnp.where(el2 == e2, lane, big), axis=-1, keepdims=True)
    t = jnp.exp(e2 - e1)
    den = 1.0 + t
    gate1 = g_p * (1.0 / den)
    gate2 = g_p * (t / den)
    hot1 = lane == i1
    hot2 = lane == i2
    onehot = jnp.where(hot1 | hot2, 1.0, 0.0)
    r = lax.broadcasted_iota(I32, (rows, rows), 0)
    cidx = lax.broadcasted_iota(I32, (rows, rows), 1)
    strict = jnp.where(cidx < r, 1.0, 0.0).astype(BF16)
    cum = _dot(strict, onehot.astype(BF16)) + cnt_ref[...]
    rank1 = jnp.sum(jnp.where(hot1, cum, 0.0), axis=-1, keepdims=True).astype(I32)
    rank2 = jnp.sum(jnp.where(hot2, cum, 0.0), axis=-1, keepdims=True).astype(I32)
    cnt_ref[...] = cnt_ref[...] + jnp.sum(onehot, axis=0, keepdims=True)
    ri_ref[...] = jnp.where(lane == 0, i1, jnp.where(lane == 1, i2,
                            jnp.where(lane == 2, rank1, jnp.where(lane == 3, rank2, 0))))
    rg_ref[...] = jnp.where(lane == 0, gate1, jnp.where(lane == 1, gate2, 0.0))


def _outproj_router_kernel(*refs, n_main, split_x, has_bias):
    it = iter(refs)
    xp_ref = next(it)
    xs_ref = next(it) if split_x else None
    op_ref, os_ref, wo_ref = next(it), next(it), next(it)
    bo_ref = next(it) if has_bias else None
    g_ref, whi_ref, wlo_ref, br_ref = next(it), next(it), next(it), next(it)
    x1_ref, xn_ref, ri_ref, rg_ref, cnt_ref = next(it), next(it), next(it), next(it), next(it)
    i = pl.program_id(0)

    @pl.when(i == 0)
    def _():
        cnt_ref[...] = jnp.zeros_like(cnt_ref)

    if split_x:
        x = jnp.where(i < n_main, xp_ref[...], xs_ref[...])
    else:
        x = xp_ref[...]
    o = jnp.where(i < n_main, op_ref[...], os_ref[...])
    x1 = x + _dot(o, wo_ref[...])
    if has_bias:
        x1 = x1 + bo_ref[...]
    x1_ref[...] = x1
    xn = _rms(x1, g_ref[...])
    xn_ref[...] = xn
    _route(xn, whi_ref, wlo_ref, br_ref, cnt_ref, ri_ref, rg_ref)


def _outproj_router(x_main, x_tail, o_main, o_tail, wo, bo, g, whi, wlo, br):
    split_x = x_tail is not None
    has_bias = bo is not None
    n_main = o_main.shape[0] // TM
    n_tiles = n_main + 1
    t_pad = n_tiles * TM
    row = lambda i: (i, 0)
    main = lambda i: (jnp.minimum(i, n_main - 1), 0)
    fixed = lambda i: (0, 0)
    full = lambda a: pl.BlockSpec(a.shape, fixed)
    args, specs = [], []
    if split_x:
        args += [x_main, x_tail]
        specs += [pl.BlockSpec((TM, D_MODEL), main), pl.BlockSpec((TM, D_MODEL), fixed)]
    else:
        args += [x_main]
        specs += [pl.BlockSpec((TM, D_MODEL), row)]
    args += [o_main, o_tail, wo]
    specs += [pl.BlockSpec((TM, D_MODEL), main), pl.BlockSpec((TM, D_MODEL), fixed), full(wo)]
    if has_bias:
        args += [bo]
        specs += [full(bo)]
    args += [g, whi, wlo, br]
    specs += [full(g), full(whi), full(wlo), full(br)]
    kern = functools.partial(_outproj_router_kernel, n_main=n_main, split_x=split_x,
                             has_bias=has_bias)
    return pl.pallas_call(
        kern,
        grid=(n_tiles,),
        in_specs=specs,
        out_specs=[pl.BlockSpec((TM, D_MODEL), row), pl.BlockSpec((TM, D_MODEL), row),
                   pl.BlockSpec((TM, LANES), row), pl.BlockSpec((TM, LANES), row),
                   pl.BlockSpec((1, LANES), fixed)],
        out_shape=[jax.ShapeDtypeStruct((t_pad, D_MODEL), F32),
                   jax.ShapeDtypeStruct((t_pad, D_MODEL), F32),
                   jax.ShapeDtypeStruct((t_pad, LANES), I32),
                   jax.ShapeDtypeStruct((t_pad, LANES), F32),
                   jax.ShapeDtypeStruct((1, LANES), F32)],
        compiler_params=_cparams(("arbitrary",)),
        name="outproj_router",
    )(*args)


def _wait_rows(make_copy, n):
    bit = MOE_BM
    while bit >= 1:
        @pl.when((n & bit) != 0)
        def _(bit=bit):
            make_copy(bit).wait()
        bit //= 2


def _moe_kernel(dest_ref, blke_ref, nval_ref, xn_ref, wg_ref, wu_ref, wd_ref, ys_ref,
                slot_ref, xbuf, obuf, wgu, wdb, gsem, ssem, *, n_slots):
    j = pl.program_id(0)
    nb = pl.num_programs(0)
    cur = j % 2

    def gather_start(blk, sl):
        base = blk * MOE_BM

        def body(r, carry):
            tok = jnp.right_shift(slot_ref[base + r], 1)
            pltpu.make_async_copy(xn_ref.at[pl.ds(tok, 1)], xbuf.at[sl, pl.ds(r, 1)],
                                  gsem.at[sl]).start()
            return carry
        lax.fori_loop(0, nval_ref[blk], body, 0)

    def gather_wait(blk, sl):
        _wait_rows(lambda n: pltpu.make_async_copy(xn_ref.at[pl.ds(0, n)],
                                                   xbuf.at[sl, pl.ds(0, n)], gsem.at[sl]),
                   nval_ref[blk])

    def scatter_start(blk, sl):
        base = blk * MOE_BM

        def body(r, carry):
            s = slot_ref[base + r]
            pltpu.make_async_copy(obuf.at[sl, pl.ds(r, 1)], ys_ref.at[pl.ds(s, 1)],
                                  ssem.at[sl]).start()
            return carry
        lax.fori_loop(0, nval_ref[blk], body, 0)

    def scatter_wait(blk, sl):
        _wait_rows(lambda n: pltpu.make_async_copy(obuf.at[sl, pl.ds(0, n)],
                                                   ys_ref.at[pl.ds(0, n)], ssem.at[sl]),
                   nval_ref[blk])

    @pl.when(j == 0)
    def _():
        def inv(s, carry):
            slot_ref[dest_ref[s]] = s
            return carry
        lax.fori_loop(0, n_slots, inv, 0)
        xbuf[...] = jnp.zeros_like(xbuf)
        gather_start(0, 0)

    @pl.when(j + 1 < nb)
    def _():
        gather_start(j + 1, 1 - cur)

    gather_wait(j, cur)

    @pl.when(jnp.logical_or(j == 0, blke_ref[j] != blke_ref[jnp.maximum(j - 1, 0)]))
    def _():
        wgu[:, :D_FF_EXPERT] = wg_ref[0].astype(BF16)
        wgu[:, D_FF_EXPERT:] = wu_ref[0].astype(BF16)
        wdb[...] = wd_ref[0].astype(BF16)

    @pl.when(j >= 2)
    def _():
        scatter_wait(j - 2, cur)

    @pl.when(nval_ref[j] > 0)
    def _():
        x = xbuf[cur].astype(BF16)
        gu = _dot(x, wgu[...])
        gt = gu[:, :D_FF_EXPERT]
        hid = gt * (1.0 / (1.0 + jnp.exp(-gt))) * gu[:, D_FF_EXPERT:]
        obuf[cur] = _dot(hid.astype(BF16), wdb[...])

    scatter_start(j, cur)

    @pl.when(j == nb - 1)
    def _():
        @pl.when(j >= 1)
        def _():
            scatter_wait(j - 1, 1 - cur)
        scatter_wait(j, cur)


def _moe_experts(dest, blk_e, nvalid, xn, wg, wu, wd):
    n_slots = dest.shape[0]
    nb = blk_e.shape[0]
    n_rows = nb * MOE_BM
    kern = functools.partial(_moe_kernel, n_slots=n_slots)
    return pl.pallas_call(
        kern,
        grid_spec=pltpu.PrefetchScalarGridSpec(
            num_scalar_prefetch=3,
            grid=(nb,),
            in_specs=[pl.BlockSpec(memory_space=pl.ANY),
                      pl.BlockSpec((1, D_MODEL, D_FF_EXPERT), lambda j, d, e, n: (e[j], 0, 0)),
                      pl.BlockSpec((1, D_MODEL, D_FF_EXPERT), lambda j, d, e, n: (e[j], 0, 0)),
                      pl.BlockSpec((1, D_FF_EXPERT, D_MODEL), lambda j, d, e, n: (e[j], 0, 0))],
            out_specs=pl.BlockSpec(memory_space=pl.ANY),
            scratch_shapes=[pltpu.SMEM((n_rows,), I32),
                            pltpu.VMEM((2, MOE_BM, D_MODEL), F32),
                            pltpu.VMEM((2, MOE_BM, D_MODEL), F32),
                            pltpu.VMEM((D_MODEL, 2 * D_FF_EXPERT), BF16),
                            pltpu.VMEM((D_FF_EXPERT, D_MODEL), BF16),
                            pltpu.SemaphoreType.DMA((2,)),
                            pltpu.SemaphoreType.DMA((2,))]),
        out_shape=jax.ShapeDtypeStruct((n_slots, D_MODEL), F32),
        compiler_params=_cparams(("arbitrary",)),
        name="moe_experts",
    )(dest, blk_e, nvalid, xn, wg, wu, wd)


def _moe_plan(ri, counts):
    eid = ri[:, 0:2]
    rank = ri[:, 2:4]
    cnt = counts[0, :N_EXPERTS].astype(I32)
    padded = (cnt + MOE_BM - 1) // MOE_BM * MOE_BM
    pad_end = jnp.cumsum(padded)
    pad_start = pad_end - padded
    dest = (pad_start[eid] + rank).reshape(-1)
    n_slots = dest.shape[0]
    nb = -(-n_slots // MOE_BM) + N_EXPERTS
    blk_row = jnp.arange(nb, dtype=I32) * MOE_BM
    blk_e = jnp.minimum(jnp.searchsorted(pad_end, blk_row, side="right"), N_EXPERTS - 1).astype(I32)
    nvalid = jnp.clip(pad_start[blk_e] + cnt[blk_e] - blk_row, 0, MOE_BM).astype(I32)
    return dest.astype(I32), blk_e, nvalid


def _combine(x1_ref, ys_ref, rg_ref):
    rg = rg_ref[...]
    ys = ys_ref[...]
    return x1_ref[...] + rg[:, 0:1] * ys[:, :D_MODEL] + rg[:, 1:2] * ys[:, D_MODEL:]


def _swa_proj_kernel(x1_ref, ys_ref, rg_ref, g_ref, w_ref, b_ref, c_ref, sa_ref, sb_ref,
                     x2_ref, qkv_ref):
    x2 = _combine(x1_ref, ys_ref, rg_ref)
    x2_ref[...] = x2
    h = _rms(x2, g_ref[...])
    qkv = _dot(h.astype(BF16), w_ref[...]) + b_ref[...]
    ct, sa, sb = c_ref[...], sa_ref[...], sb_ref[...]
    nq = SWA_HEADS * SWA_HEAD_DIM
    nk = SWA_KV_HEADS * LANES
    for ch in range(nq // LANES):
        lo = ch * LANES
        qkv_ref[:, lo:lo + LANES] = (_rope(qkv[:, lo:lo + LANES], ct, sa, sb) * SWA_SCALE).astype(BF16)
    for ch in range(SWA_KV_HEADS):
        lo = nq + ch * LANES
        qkv_ref[:, lo:lo + LANES] = _rope(qkv[:, lo:lo + LANES], ct, sa, sb).astype(BF16)
    qkv_ref[:, nq + nk:] = qkv[:, nq + nk:].astype(BF16)


def _swa_proj(x1, ys2, rg, g, w, b, ct, sa, sb):
    t_pad = x1.shape[0]
    n_tiles = t_pad // TM
    row = lambda i: (i, 0)
    fixed = lambda i: (0, 0)
    full = lambda a: pl.BlockSpec(a.shape, fixed)
    width = w.shape[1]
    return pl.pallas_call(
        _swa_proj_kernel,
        grid=(n_tiles,),
        in_specs=[pl.BlockSpec((TM, D_MODEL), row), pl.BlockSpec((TM, 2 * D_MODEL), row),
                  pl.BlockSpec((TM, LANES), row), full(g), full(w), full(b),
                  pl.BlockSpec((TM, LANES), row), pl.BlockSpec((TM, LANES), row),
                  pl.BlockSpec((TM, LANES), row)],
        out_specs=[pl.BlockSpec((TM, D_MODEL), row), pl.BlockSpec((TM, width), row)],
        out_shape=[jax.ShapeDtypeStruct((t_pad, D_MODEL), F32),
                   jax.ShapeDtypeStruct((t_pad, width), BF16)],
        compiler_params=_cparams(("arbitrary",)),
        name="swa_proj",
    )(x1, ys2, rg, g, w, b, ct, sa, sb)


def _swa_kv_rows_kernel(blk_ref, x_ref, g_ref, w_ref, b_ref, c_ref, sa_ref, sb_ref, o_ref):
    del blk_ref
    h = _rms(x_ref[...], g_ref[...])
    kv = _dot(h.astype(BF16), w_ref[...]) + b_ref[...]
    ct, sa, sb = c_ref[...], sa_ref[...], sb_ref[...]
    nk = SWA_KV_HEADS * SWA_HEAD_DIM
    for ch in range(nk // LANES):
        lo = ch * LANES
        o_ref[:, lo:lo + LANES] = _rope(kv[:, lo:lo + LANES], ct, sa, sb)
    o_ref[:, nk:] = kv[:, nk:]


def _swa_kv_rows(x2, g, w, b, ct, sa, sb, blocks):
    nblk = blocks.shape[0]
    rb = WINDOW
    pick = lambda i, blk: (blk[i], 0)
    fixed = lambda i, blk: (0, 0)
    full = lambda a: pl.BlockSpec(a.shape, fixed)
    return pl.pallas_call(
        _swa_kv_rows_kernel,
        grid_spec=pltpu.PrefetchScalarGridSpec(
            num_scalar_prefetch=1,
            grid=(nblk,),
            in_specs=[pl.BlockSpec((rb, D_MODEL), pick), full(g), full(w), full(b),
                      pl.BlockSpec((rb, LANES), pick), pl.BlockSpec((rb, LANES), pick),
                      pl.BlockSpec((rb, LANES), pick)],
            out_specs=pl.BlockSpec((rb, w.shape[1]), lambda i, blk: (i, 0))),
        out_shape=jax.ShapeDtypeStruct((nblk * rb, w.shape[1]), F32),
        compiler_params=_cparams(("arbitrary",)),
        name="swa_kv_rows",
    )(blocks, x2, g, w, b, ct, sa, sb)


def _sink_softmax(s, sink):
    m = jnp.maximum(jnp.max(s, axis=-1, keepdims=True), sink)
    e = jnp.exp(s - m)
    return e / (jnp.sum(e, axis=-1, keepdims=True) + jnp.exp(sink - m))


def _swa_attn_kernel(sink_ref, q_ref, k_ref, v_ref, o_ref, *, n_blocks):
    g = pl.program_id(1)
    w = WINDOW

    def block(qs, ks, nkeys, first):
        qb = q_ref[pl.ds(qs, w), :]
        kk = k_ref[pl.ds(ks, nkeys), :]
        vv = v_ref[pl.ds(ks, nkeys), :]
        lane = lax.broadcasted_iota(I32, (nkeys, LANES), 1)
        zero = jnp.zeros_like(kk)
        kp = (jnp.where(lane < SWA_HEAD_DIM, kk, zero), jnp.where(lane >= SWA_HEAD_DIM, kk, zero))
        vp = (jnp.where(lane < SWA_HEAD_DIM, vv, zero), jnp.where(lane >= SWA_HEAD_DIM, vv, zero))
        r = lax.broadcasted_iota(I32, (w, nkeys), 0)
        cidx = lax.broadcasted_iota(I32, (w, nkeys), 1)
        valid = (cidx <= r) if first else ((cidx > r) & (cidx <= r + w))
        for ch in range(2):
            qc = qb[:, ch * LANES:(ch + 1) * LANES]
            acc = None
            for half in range(2):
                s = jnp.where(valid, _dot_nt(qc, kp[half]), -jnp.inf)
                p = _sink_softmax(s, sink_ref[g * 4 + ch * 2 + half])
                part = _dot(p.astype(BF16), vp[half])
                acc = part if acc is None else acc + part
            o_ref[pl.ds(qs, w), ch * LANES:(ch + 1) * LANES] = acc.astype(BF16)

    block(0, 0, w, True)

    def body(i, carry):
        qs = pl.multiple_of(i * w, w)
        block(qs, pl.multiple_of(qs - w, w), 2 * w, False)
        return carry

    lax.fori_loop(1, n_blocks, body, 0)


def _swa_attn_prompt(sinks, qkv, batch, seq):
    nq = SWA_HEADS * SWA_HEAD_DIM
    qw = nq // SWA_KV_HEADS
    k0 = nq // LANES
    v0 = k0 + SWA_KV_HEADS
    kern = functools.partial(_swa_attn_kernel, n_blocks=seq // WINDOW)
    return pl.pallas_call(
        kern,
        grid=(batch, SWA_KV_HEADS),
        in_specs=[pl.BlockSpec(memory_space=pltpu.SMEM),
                  pl.BlockSpec((seq, qw), lambda b, g: (b, g)),
                  pl.BlockSpec((seq, LANES), lambda b, g: (b, k0 + g)),
                  pl.BlockSpec((seq, LANES), lambda b, g: (b, v0 + g))],
        out_specs=pl.BlockSpec((seq, qw), lambda b, g: (b, g)),
        out_shape=jax.ShapeDtypeStruct((batch * seq, nq), BF16),
        compiler_params=_cparams(("arbitrary", "arbitrary")),
        name="swa_attn_prompt",
    )(sinks, qkv, qkv, qkv)


def _swa_sample_kernel(sink_ref, q_ref, knew_ref, vnew_ref, cache_ref, o_ref, *, dec_seq):
    w = WINDOW
    nk = SWA_KV_HEADS * SWA_HEAD_DIM
    rows = 4 * dec_seq
    lane = lax.broadcasted_iota(I32, (w, LANES), 1)
    r = lax.broadcasted_iota(I32, (rows, 2 * w), 0)
    cidx = lax.broadcasted_iota(I32, (rows, 2 * w), 1)
    tok = jnp.bitwise_and(r, dec_seq - 1)
    valid = (cidx > tok) & (cidx <= tok + w)
    zpad = jnp.zeros((w - 16, LANES), BF16)
    for g in range(SWA_KV_HEADS):
        j = g // 2
        kc = cache_ref[0, :, j * LANES:(j + 1) * LANES]
        vc = cache_ref[0, :, nk + j * LANES:nk + (j + 1) * LANES]
        low = (lane < SWA_HEAD_DIM) if g % 2 == 0 else (lane >= SWA_HEAD_DIM)
        kd = jnp.where(low, kc, pltpu.roll(kc, SWA_HEAD_DIM, 1)).astype(BF16)
        vd = jnp.where(low, vc, pltpu.roll(vc, SWA_HEAD_DIM, 1)).astype(BF16)
        k_all = jnp.concatenate([kd, knew_ref[0, g], zpad], axis=0)
        v_all = jnp.concatenate([vd, vnew_ref[0, g], zpad], axis=0)
        lhs = q_ref[0, g]
        s = jnp.where(valid, _dot_nt(lhs, k_all), -jnp.inf)
        hrow = jnp.right_shift(lax.broadcasted_iota(I32, (rows, 1), 0), 2)
        sink = jnp.where(hrow == 0, sink_ref[g * 4],
                         jnp.where(hrow == 1, sink_ref[g * 4 + 1],
                                   jnp.where(hrow == 2, sink_ref[g * 4 + 2], sink_ref[g * 4 + 3])))
        p = _sink_softmax(s, sink)
        o_ref[0, g] = _dot(p.astype(BF16), v_all)


def _swa_attn_sample(sinks, qs, knew, vnew, cache):
    dec_batch = qs.shape[0]
    dec_seq = qs.shape[2] // 4
    kern = functools.partial(_swa_sample_kernel, dec_seq=dec_seq)
    blk4 = lambda a: pl.BlockSpec((1,) + a.shape[1:], lambda b: (b, 0, 0, 0))
    return pl.pallas_call(
        kern,
        grid=(dec_batch,),
        in_specs=[pl.BlockSpec(memory_space=pltpu.SMEM), blk4(qs), blk4(knew), blk4(vnew),
                  pl.BlockSpec((1,) + cache.shape[1:], lambda b: (b, 0, 0))],
        out_specs=pl.BlockSpec((1, SWA_KV_HEADS, 4 * dec_seq, LANES), lambda b: (b, 0, 0, 0)),
        out_shape=jax.ShapeDtypeStruct((dec_batch, SWA_KV_HEADS, 4 * dec_seq, LANES), F32),
        compiler_params=_cparams(("arbitrary",)),
        name="swa_attn_sample",
    )(sinks, qs, knew, vnew, cache)


def _final_kernel(x1_ref, ys_ref, rg_ref, g_ref, yp_ref, ys_out_ref, *, n_main):
    i = pl.program_id(0)
    y = _rms(_combine(x1_ref, ys_ref, rg_ref), g_ref[...])

    @pl.when(i < n_main)
    def _():
        yp_ref[...] = y

    @pl.when(i >= n_main)
    def _():
        ys_out_ref[...] = y


def _final(x1, ys2, rg, g):
    t_pad = x1.shape[0]
    n_tiles = t_pad // TM
    n_main = n_tiles - 1
    row = lambda i: (i, 0)
    main = lambda i: (jnp.minimum(i, n_main - 1), 0)
    fixed = lambda i: (0, 0)
    return pl.pallas_call(
        functools.partial(_final_kernel, n_main=n_main),
        grid=(n_tiles,),
        in_specs=[pl.BlockSpec((TM, D_MODEL), row), pl.BlockSpec((TM, 2 * D_MODEL), row),
                  pl.BlockSpec((TM, LANES), row), pl.BlockSpec(g.shape, fixed)],
        out_specs=[pl.BlockSpec((TM, D_MODEL), main), pl.BlockSpec((TM, D_MODEL), fixed)],
        out_shape=[jax.ShapeDtypeStruct((n_main * TM, D_MODEL), F32),
                   jax.ShapeDtypeStruct((TM, D_MODEL), F32)],
        compiler_params=_cparams(("arbitrary",)),
        name="final_norm",
    )(x1, ys2, rg, g)


def _rope_tables(pos, paired):
    half = QK_ROPE // 2
    inv_freq = ROPE_THETA ** (-jnp.arange(half, dtype=F32) / half)
    ang = pos.astype(F32)[:, None] * inv_freq[None, :]
    cos, sin = jnp.cos(ang), jnp.sin(ang)
    z = jnp.zeros_like(cos)
    if paired:
        ct = jnp.concatenate([cos, cos, cos, cos], axis=1)
        sa = jnp.concatenate([-sin, z, -sin, z], axis=1)
        sb = jnp.concatenate([z, sin, z, sin], axis=1)
    else:
        ct = jnp.concatenate([cos, cos, z, z], axis=1)
        sa = jnp.concatenate([-sin, z, z, z], axis=1)
        sb = jnp.concatenate([z, sin, z, z], axis=1)
    return ct, sa, sb


def _router_weights(w_group, b_group, w_expert, b_expert):
    w = jnp.concatenate([w_expert, w_group,
                         jnp.zeros((D_MODEL, LANES - N_EXPERTS - N_GROUPS), F32)], axis=1)
    b = jnp.concatenate([b_expert, b_group, jnp.zeros((LANES - N_EXPERTS - N_GROUPS,), F32)])
    whi = w.astype(BF16)
    wlo = (w - whi.astype(F32)).astype(BF16)
    return whi, wlo, b.reshape(1, LANES)


def kernel(x_prompt, x_sample, cache_mla, cache_swa_kv, page_table, norm_mix, norm_ffn, norm_final, mla_w_a, mla_q_norm, mla_wq_b, mla_kv_norm, mla_wkv_b, mla_wo, swa_wqkv, swa_bqkv, swa_sinks, swa_wo, swa_bo, moe_w_group, moe_b_group, moe_w_expert, moe_b_expert, moe_w_gate, moe_w_up, moe_w_down):
    batch, seq, _ = x_prompt.shape
    dec_batch, dec_seq, _ = x_sample.shape
    n_prompt = batch * seq
    n_sample = dec_batch * dec_seq
    past_len = page_table.shape[1] * PAGE_SIZE
    assert n_prompt % TM == 0 and seq % MLA_TQ == 0 and n_sample <= TM and dec_seq == 4
    assert page_table.shape[1] % PAGES_PER_STEP == 0
    n_main = n_prompt // TM
    t_pad = n_prompt + TM
    pad_rows = TM - n_sample

    xp = x_prompt.reshape(n_prompt, D_MODEL)
    xs = jnp.pad(x_sample.reshape(n_sample, D_MODEL), ((0, pad_rows), (0, 0)))
    pos = jnp.concatenate([jnp.tile(jnp.arange(seq, dtype=I32), batch),
                           jnp.tile(past_len + jnp.arange(dec_seq, dtype=I32), dec_batch),
                           jnp.zeros((pad_rows,), I32)])
    mla_tabs = _rope_tables(pos, paired=False)
    swa_tabs = _rope_tables(pos, paired=True)
    row2 = lambda v: v.reshape(1, -1)

    wa = jnp.pad(mla_w_a[0], ((0, 0), (0, LANES - QK_ROPE))).astype(BF16)
    wq = jnp.pad(mla_wq_b[0].reshape(Q_LORA, MLA_HEADS, QK_NOPE + QK_ROPE),
                 ((0, 0), (0, 0), (0, 2 * LANES - QK_NOPE - QK_ROPE))
                 ).reshape(Q_LORA, MLA_HEADS * 2 * LANES).astype(BF16)
    wkv = mla_wkv_b[0].reshape(KV_LORA, MLA_HEADS, QK_NOPE + V_HEAD)
    w_uk = wkv[..., :QK_NOPE]
    w_uv = wkv[..., QK_NOPE:]
    wuk2 = w_uk.reshape(KV_LORA, MLA_HEADS * QK_NOPE).astype(BF16)
    wuv2 = w_uv.reshape(KV_LORA, MLA_HEADS * V_HEAD).astype(BF16)
    w_ukt = jnp.transpose(w_uk, (1, 2, 0)).astype(BF16)
    w_uvh = jnp.transpose(w_uv, (1, 0, 2)).astype(BF16)

    q, kn, kr, v, rows_p, rows_s = _mla_proj(
        xp, xs, row2(norm_mix[0]), wa, row2(mla_q_norm[0]), wq, row2(mla_kv_norm[0]), wuk2, wuv2,
        *mla_tabs)
    o_p = _mla_attn_prompt(q, kn, kr, v, batch, seq)

    qa = _mla_q_absorb(q, w_ukt, n_main)
    rws = MLA_HEADS * dec_seq
    qcat = jnp.transpose(qa[:, :n_sample].reshape(MLA_HEADS, dec_batch, dec_seq, -1),
                         (1, 0, 2, 3)).reshape(dec_batch, rws, KV_LORA + LANES)
    new_rows = rows_s[:n_sample].reshape(dec_batch, dec_seq, MLA_ROW)
    o_lat = _mla_attn_sample(page_table, qcat, new_rows, cache_mla[0])
    o_lat = jnp.transpose(o_lat.reshape(dec_batch, MLA_HEADS, dec_seq, KV_LORA), (1, 0, 2, 3))
    o_lat = jnp.pad(o_lat.reshape(MLA_HEADS, n_sample, KV_LORA), ((0, 0), (0, pad_rows), (0, 0)))
    o_s = _mla_v_up(o_lat, w_uvh)

    whi, wlo, br = _router_weights(moe_w_group[0], moe_b_group[0], moe_w_expert[0], moe_b_expert[0])
    x1, xn, ri, rg, counts = _outproj_router(
        xp, xs, o_p, o_s, mla_wo[0].astype(BF16), None, row2(norm_ffn[0]), whi, wlo, br)
    dest, blk_e, nvalid = _moe_plan(ri, counts)
    ys = _moe_experts(dest, blk_e, nvalid, xn, moe_w_gate[0], moe_w_up[0], moe_w_down[0])

    nq = SWA_HEADS * SWA_HEAD_DIM
    nk = SWA_KV_HEADS * SWA_HEAD_DIM
    wqkv = swa_wqkv[0]
    dup = lambda m: jnp.concatenate(
        [m.reshape(m.shape[0], SWA_KV_HEADS, 1, SWA_HEAD_DIM)] * 2, axis=2
    ).reshape(m.shape[0], SWA_KV_HEADS * LANES)
    w1 = jnp.concatenate([wqkv[:, :nq], dup(wqkv[:, nq:nq + nk]), dup(wqkv[:, nq + nk:])], axis=1)
    bq = swa_bqkv[0].reshape(1, -1)
    b1 = jnp.concatenate([bq[:, :nq], dup(bq[:, nq:nq + nk]), dup(bq[:, nq + nk:])], axis=1)
    x2, qkv = _swa_proj(x1, ys.reshape(t_pad, 2 * D_MODEL), rg, row2(norm_mix[1]),
                        w1.astype(BF16), b1, *swa_tabs)

    nwb = seq // WINDOW
    blocks = jnp.concatenate([jnp.arange(batch, dtype=I32) * nwb + (nwb - 1),
                              jnp.full((1,), n_prompt // WINDOW, I32)])
    kvf = _swa_kv_rows(x2, row2(norm_mix[1]), wqkv[:, nq:].astype(BF16), bq[:, nq:],
                       *swa_tabs, blocks)
    kv_p = kvf[:batch * WINDOW].reshape(batch, WINDOW, 2, SWA_KV_HEADS, SWA_HEAD_DIM)
    kv_new = kvf[batch * WINDOW:batch * WINDOW + n_sample].reshape(
        dec_batch, dec_seq, 2, SWA_KV_HEADS, SWA_HEAD_DIM)
    swa_kv_prompt = kv_p[None]
    swa_kv_sample = jnp.concatenate([cache_swa_kv[0][:, dec_seq:], kv_new], axis=1)[None]

    sinks = swa_sinks[0]
    o_p = _swa_attn_prompt(sinks, qkv, batch, seq)

    qkv_s = qkv[n_prompt:n_prompt + n_sample]
    qh = qkv_s[:, :nq].reshape(dec_batch, dec_seq, SWA_KV_HEADS, 2, 2, SWA_HEAD_DIM)
    qh = jnp.transpose(qh, (0, 2, 3, 4, 1, 5))
    zq = jnp.zeros_like(qh)
    qlo = jnp.concatenate([qh[:, :, :, 0], zq[:, :, :, 0]], axis=-1)
    qhi = jnp.concatenate([zq[:, :, :, 1], qh[:, :, :, 1]], axis=-1)
    qs = jnp.stack([qlo, qhi], axis=3).reshape(dec_batch, SWA_KV_HEADS, 4 * dec_seq, LANES)
    padk = lambda m: jnp.pad(
        jnp.transpose(m.reshape(dec_batch, dec_seq, SWA_KV_HEADS, LANES), (0, 2, 1, 3)),
        ((0, 0), (0, 0), (0, 16 - dec_seq), (0, 0)))
    knew = padk(qkv_s[:, nq:nq + SWA_KV_HEADS * LANES])
    vnew = padk(qkv_s[:, nq + SWA_KV_HEADS * LANES:])
    cache_s = cache_swa_kv[0].reshape(dec_batch, WINDOW, 2 * nk)
    o_raw = _swa_attn_sample(sinks, qs, knew, vnew, cache_s)
    o_raw = o_raw.reshape(dec_batch, SWA_KV_HEADS, 2, 2, dec_seq, 2, SWA_HEAD_DIM)
    o_sel = jnp.stack([o_raw[:, :, :, 0, :, 0], o_raw[:, :, :, 1, :, 1]], axis=3)
    o_s = jnp.transpose(o_sel, (0, 4, 1, 2, 3, 5)).reshape(n_sample, nq)
    o_s = jnp.pad(o_s, ((0, pad_rows), (0, 0))).astype(BF16)

    whi, wlo, br = _router_weights(moe_w_group[1], moe_b_group[1], moe_w_expert[1], moe_b_expert[1])
    x3, xn, ri, rg, counts = _outproj_router(
        x2, None, o_p, o_s, swa_wo[0].astype(BF16), row2(swa_bo[0]), row2(norm_ffn[1]),
        whi, wlo, br)
    dest, blk_e, nvalid = _moe_plan(ri, counts)
    ys = _moe_experts(dest, blk_e, nvalid, xn, moe_w_gate[1], moe_w_up[1], moe_w_down[1])

    y_p, y_s = _final(x3, ys.reshape(t_pad, 2 * D_MODEL), rg, row2(norm_final))

    y_prompt = y_p.reshape(batch, seq, D_MODEL)
    y_sample = y_s[:n_sample].reshape(dec_batch, dec_seq, D_MODEL)
    mla_rows_prompt = rows_p.reshape(1, batch, seq, MLA_ROW)
    mla_rows_sample = rows_s[:n_sample].reshape(1, dec_batch, dec_seq, MLA_ROW)
    return (y_prompt, y_sample, mla_rows_prompt, mla_rows_sample, swa_kv_prompt, swa_kv_sample)
```
